```python
import jax, jax.numpy as jnp
from jax import lax
import numpy as np

D_MODEL = 1024
BATCH = 8
SEQ = 4096
DEPTH = 1

CONV_WIDTH = D_MODEL
CONV_GROUPS = 8
CONV_KERNEL = 31
SGU_WIDTH = D_MODEL
SGU_HEADS = 8
SGU_HEAD_DIM = SGU_WIDTH // SGU_HEADS
CHUNK = 128
EPS = 1e-6

OFF_A_VAL = 0
OFF_A_GLU = OFF_A_VAL + CONV_WIDTH
OFF_A_SILU = OFF_A_GLU + CONV_WIDTH
OFF_B_U = OFF_A_SILU + CONV_WIDTH
OFF_B_V = OFF_B_U + SGU_WIDTH
OFF_B_SILU = OFF_B_V + SGU_WIDTH
OFF_G_A = OFF_B_SILU + SGU_WIDTH
OFF_G_B = OFF_G_A + D_MODEL
IN_COLS = OFF_G_B + D_MODEL

kernel_name = "hybrid_conformer_conv_gmlp_adaln"


def rmsnorm(x, g):
    x32 = x.astype(jnp.float32)
    y = x32 * lax.rsqrt(jnp.mean(x32 * x32, axis=-1, keepdims=True) + EPS)
    return y.astype(x.dtype) * g


def layernorm(x, g, b):
    x32 = x.astype(jnp.float32)
    mu = jnp.mean(x32, axis=-1, keepdims=True)
    xc = x32 - mu
    var = jnp.mean(xc * xc, axis=-1, keepdims=True)
    return (xc * lax.rsqrt(var + EPS)).astype(x.dtype) * g + b


def conformer_conv_branch(val, glu, z, conv_w, conv_b, ln_g, ln_b, w_out):
    a = val * jax.nn.sigmoid(glu)
    kern = conv_w.reshape(CONV_KERNEL, 1, CONV_WIDTH)
    y = lax.conv_general_dilated(
        a, kern, window_strides=(1,), padding=[(CONV_KERNEL - 1, 0)],
        dimension_numbers=("NWC", "WIO", "NWC"),
        feature_group_count=CONV_WIDTH) + conv_b
    y = jax.nn.silu(layernorm(y, ln_g, ln_b))
    y = y * jax.nn.silu(z)
    return y @ w_out


def sgu_branch(u, v, z, ln_g, ln_b, w_s, b_s, w_out):
    bsz, seq, _ = u.shape
    u = jax.nn.gelu(u, approximate=False)
    v = layernorm(jax.nn.gelu(v, approximate=False), ln_g, ln_b)
    vc = v.reshape(bsz, seq // CHUNK, CHUNK, SGU_HEADS, SGU_HEAD_DIM)
    causal = jnp.tril(jnp.ones((CHUNK, CHUNK), dtype=bool))
    ws = jnp.where(causal[None], w_s, jnp.zeros((), w_s.dtype))
    s = jnp.einsum("hts,bcshd->bcthd", ws, vc) + b_s.T[:, :, None]
    s = s.reshape(bsz, seq, SGU_WIDTH)
    y = u * s * jax.nn.silu(z)
    return y @ w_out


def setup_inputs(seed: int = 0) -> dict:
    key = jax.random.key(seed)
    ks = jax.random.split(key, 20)
    f32 = jnp.float32
    n = lambda k, shape, s: (jax.random.normal(k, shape, f32) * s)
    x = jax.random.normal(ks[0], (BATCH, SEQ, D_MODEL), f32)
    c = jax.random.normal(ks[1], (BATCH, D_MODEL), f32)
    w_ada = n(ks[2], (DEPTH, D_MODEL, 3 * D_MODEL), 0.3 * D_MODEL ** -0.5)
    b_ada = n(ks[3], (DEPTH, 3 * D_MODEL), 0.02)
    g_pre = 1.0 + n(ks[4], (DEPTH, D_MODEL), 0.02)
    w_in = n(ks[5], (DEPTH, D_MODEL, IN_COLS), D_MODEL ** -0.5)
    conv_w = n(ks[6], (DEPTH, CONV_KERNEL, CONV_WIDTH), CONV_KERNEL ** -0.5)
    conv_b = n(ks[7], (DEPTH, CONV_WIDTH), 0.02)
    conv_ln_g = 1.0 + n(ks[8], (DEPTH, CONV_WIDTH), 0.02)
    conv_ln_b = n(ks[9], (DEPTH, CONV_WIDTH), 0.02)
    w_conv_out = n(ks[10], (DEPTH, CONV_WIDTH, D_MODEL), CONV_WIDTH ** -0.5)
    sgu_ln_g = 1.0 + n(ks[11], (DEPTH, SGU_WIDTH), 0.02)
    sgu_ln_b = n(ks[12], (DEPTH, SGU_WIDTH), 0.02)
    w_sgu = n(ks[13], (DEPTH, SGU_HEADS, CHUNK, CHUNK), 0.5 * CHUNK ** -0.5)
    b_sgu = 1.0 + n(ks[14], (DEPTH, SGU_HEADS, CHUNK), 0.02)
    w_sgu_out = n(ks[15], (DEPTH, SGU_WIDTH, D_MODEL), SGU_WIDTH ** -0.5)
    w_o = n(ks[16], (DEPTH, D_MODEL, D_MODEL), D_MODEL ** -0.5)
    g_final = 1.0 + n(ks[17], (D_MODEL,), 0.02)
    return {"x": x, "c": c, "w_ada": w_ada, "b_ada": b_ada, "g_pre": g_pre, "w_in": w_in,
            "conv_w": conv_w, "conv_b": conv_b, "conv_ln_g": conv_ln_g, "conv_ln_b": conv_ln_b,
            "w_conv_out": w_conv_out, "sgu_ln_g": sgu_ln_g, "sgu_ln_b": sgu_ln_b,
            "w_sgu": w_sgu, "b_sgu": b_sgu, "w_sgu_out": w_sgu_out, "w_o": w_o,
            "g_final": g_final}


def reference(x, c, w_ada, b_ada, g_pre, w_in, conv_w, conv_b, conv_ln_g, conv_ln_b,
              w_conv_out, sgu_ln_g, sgu_ln_b, w_sgu, b_sgu, w_sgu_out, w_o, g_final):
    for l in range(DEPTH):
        mod = c @ w_ada[l] + b_ada[l]
        shift, scale, gate = jnp.split(mod, 3, axis=-1)
        h = rmsnorm(x, g_pre[l]) * (1.0 + scale[:, None, :]) + shift[:, None, :]
        p = h @ w_in[l]
        y_a = conformer_conv_branch(
            p[..., OFF_A_VAL:OFF_A_GLU], p[..., OFF_A_GLU:OFF_A_SILU], p[..., OFF_A_SILU:OFF_B_U],
            conv_w[l], conv_b[l], conv_ln_g[l], conv_ln_b[l], w_conv_out[l])
        y_b = sgu_branch(
            p[..., OFF_B_U:OFF_B_V], p[..., OFF_B_V:OFF_B_SILU], p[..., OFF_B_SILU:OFF_G_A],
            sgu_ln_g[l], sgu_ln_b[l], w_sgu[l], b_sgu[l], w_sgu_out[l])
        merged = (jax.nn.sigmoid(p[..., OFF_G_A:OFF_G_B]) * y_a
                  + jax.nn.sigmoid(p[..., OFF_G_B:IN_COLS]) * y_b)
        x = x + gate[:, None, :] * (merged @ w_o[l])
    return rmsnorm(x, g_final)
```

```python
import functools

import jax
import jax.numpy as jnp
from jax import lax
from jax.experimental import pallas as pl
from jax.experimental.pallas import tpu as pltpu

D_MODEL = 1024
CONV_KERNEL = 31
SGU_HEADS = 8
CHUNK = 128
EPS = 1e-6

LANES = 128
SUBLANES = 8
VMEM_LIMIT_BYTES = 58 * 1024 * 1024

SEQ_TILE = 512
HALO = 32
SLABS = D_MODEL // LANES
SEG = SEQ_TILE // SUBLANES + 1
CONV_ROWS = SEG * SUBLANES
A_ROWS = -(-(CONV_ROWS + HALO) // SUBLANES) * SUBLANES
CONV_JB = 13
assert SEG % CONV_JB == 0 and HALO >= CONV_KERNEL - 1

OFF_A_VAL, OFF_A_GLU, OFF_A_SILU, OFF_B_U, OFF_B_V, OFF_B_SILU, OFF_G_A, OFF_G_B = (
    i * D_MODEL for i in range(8))
COLS = 256
ROWS = 32


def _sigmoid(x):
    return jax.nn.sigmoid(x)


def _silu(x):
    return x * jax.nn.sigmoid(x)


def _gelu(x):
    return 0.5 * x * (1.0 + lax.erf(x * (2.0 ** -0.5)))


def _layernorm(y, g, b):
    mu = jnp.mean(y, axis=-1, keepdims=True)
    yc = y - mu
    var = jnp.mean(yc * yc, axis=-1, keepdims=True)
    return yc * lax.rsqrt(var + EPS) * g + b


def _row_loop(n_rows, block, body):
    def step(i, carry):
        body(pl.multiple_of(i * block, block))
        return carry
    lax.fori_loop(0, n_rows // block, step, None)


def _adaln_kernel(c_ref, w_ref, b_ref, o_ref):
    o_ref[...] = jnp.dot(c_ref[...], w_ref[...], preferred_element_type=jnp.float32,
                         precision=lax.Precision.HIGHEST) + b_ref[...]


def _layer_kernel(x_ref, shift_ref, scale_ref, gate_ref, gpre_ref, win_ref, cw_ref, cb_ref,
                  clg_ref, clb_ref, wco_ref, slg_ref, slb_ref, ws_ref, bs_ref, wso_ref, wo_ref,
                  gfin_ref, o_ref,
                  h_ref, a_ref, yc_ref, f0_ref, f1_ref, f2_ref, g_ref, vn_ref, wt_ref):
    s_idx = pl.program_id(1)
    T = SEQ_TILE
    f32, bf16 = jnp.float32, jnp.bfloat16

    def proj(col, width=COLS):
        return jnp.dot(h_ref[...], win_ref[:, col:col + width], preferred_element_type=f32)

    @pl.when(s_idx == 0)
    def _():
        a_ref[:, 0:HALO, :] = jnp.zeros((SLABS, HALO, LANES), f32)
        a_ref[:, HALO + T:A_ROWS, :] = jnp.zeros((SLABS, A_ROWS - HALO - T, LANES), f32)

    @pl.when(s_idx > 0)
    def _():
        a_ref[:, 0:HALO, :] = a_ref[:, T:T + HALO, :]

    row = lax.broadcasted_iota(jnp.int32, (CHUNK, CHUNK), 0)
    col = lax.broadcasted_iota(jnp.int32, (CHUNK, CHUNK), 1)
    for hd in range(SGU_HEADS):
        wt_ref[hd] = jnp.where(row >= col, ws_ref[hd], 0.0).astype(bf16)

    def prenorm(r):
        x = x_ref[pl.ds(r, ROWS), :]
        y = x * lax.rsqrt(jnp.mean(x * x, axis=-1, keepdims=True) + EPS)
        h = y * gpre_ref[...] * (1.0 + scale_ref[...]) + shift_ref[...]
        h_ref[pl.ds(r, ROWS), :] = h.astype(bf16)
    _row_loop(T, ROWS, prenorm)

    for j in range(D_MODEL // COLS):
        a = proj(OFF_A_VAL + j * COLS) * _sigmoid(proj(OFF_A_GLU + j * COLS))
        for q in range(COLS // LANES):
            a_ref[j * (COLS // LANES) + q, HALO:HALO + T, :] = a[:, q * LANES:(q + 1) * LANES]

    def conv_slab(cb, carry):
        taps = [cw_ref[cb, pl.ds(k, SUBLANES, stride=0), :] for k in range(CONV_KERNEL)]
        bias = cb_ref[cb, pl.ds(0, SUBLANES, stride=0), :]
        for j0 in range(0, SEG, CONV_JB):
            accs = [bias] * CONV_JB
            for k in range(CONV_KERNEL):
                for jj in range(CONV_JB):
                    start = j0 + jj + HALO - (CONV_KERNEL - 1) + k
                    accs[jj] = accs[jj] + taps[k] * a_ref[cb, pl.ds(start, SUBLANES, stride=SEG), :]
            for jj in range(CONV_JB):
                yc_ref[cb, pl.ds(j0 + jj, SUBLANES, stride=SEG), :] = accs[jj]
        return carry
    lax.fori_loop(0, SLABS, conv_slab, None)

    for j in range(D_MODEL // COLS):
        f0_ref[:, j * COLS:(j + 1) * COLS] = proj(OFF_A_SILU + j * COLS)

    def conv_post(r):
        y = jnp.concatenate([yc_ref[cb, pl.ds(r, ROWS), :] for cb in range(SLABS)], axis=-1)
        y = _silu(_layernorm(y, clg_ref[...], clb_ref[...]))
        g_ref[pl.ds(r, ROWS), :] = (y * _silu(f0_ref[pl.ds(r, ROWS), :])).astype(bf16)
    _row_loop(T, ROWS, conv_post)

    for j in range(D_MODEL // COLS):
        f0_ref[:, j * COLS:(j + 1) * COLS] = jnp.dot(
            g_ref[...], wco_ref[:, j * COLS:(j + 1) * COLS], preferred_element_type=f32)

    for j in range(D_MODEL // COLS):
        f1_ref[:, j * COLS:(j + 1) * COLS] = _gelu(proj(OFF_B_U + j * COLS))
        f2_ref[:, j * COLS:(j + 1) * COLS] = _gelu(proj(OFF_B_V + j * COLS))

    def sgu_norm(r):
        v = _layernorm(f2_ref[pl.ds(r, ROWS), :], slg_ref[...], slb_ref[...])
        vn_ref[pl.ds(r, ROWS), :] = v.astype(bf16)
    _row_loop(T, ROWS, sgu_norm)

    for j in range(D_MODEL // COLS):
        f2_ref[:, j * COLS:(j + 1) * COLS] = _silu(proj(OFF_B_SILU + j * COLS))

    def sgu_chunk(r):
        for hd in range(SGU_HEADS):
            cs = slice(hd * LANES, (hd + 1) * LANES)
            mix = jnp.dot(wt_ref[hd], vn_ref[pl.ds(r, CHUNK), cs], preferred_element_type=f32)
            mix = mix + bs_ref[:, cs]
            y = f1_ref[pl.ds(r, CHUNK), cs] * mix * f2_ref[pl.ds(r, CHUNK), cs]
            g_ref[pl.ds(r, CHUNK), cs] = y.astype(bf16)
    _row_loop(T, CHUNK, sgu_chunk)

    for j in range(D_MODEL // COLS):
        f1_ref[:, j * COLS:(j + 1) * COLS] = jnp.dot(
            g_ref[...], wso_ref[:, j * COLS:(j + 1) * COLS], preferred_element_type=f32)

    for j in range(D_MODEL // COLS):
        cs = slice(j * COLS, (j + 1) * COLS)
        merged = (_sigmoid(proj(OFF_G_A + j * COLS)) * f0_ref[:, cs]
                  + _sigmoid(proj(OFF_G_B + j * COLS)) * f1_ref[:, cs])
        g_ref[:, cs] = merged.astype(bf16)

    for j in range(D_MODEL // COLS):
        f2_ref[:, j * COLS:(j + 1) * COLS] = jnp.dot(
            g_ref[...], wo_ref[:, j * COLS:(j + 1) * COLS], preferred_element_type=f32)

    def final(r):
        x = x_ref[pl.ds(r, ROWS), :] + gate_ref[...] * f2_ref[pl.ds(r, ROWS), :]
        y = x * lax.rsqrt(jnp.mean(x * x, axis=-1, keepdims=True) + EPS)
        o_ref[pl.ds(r, ROWS), :] = y * gfin_ref[...]
    _row_loop(T, ROWS, final)


def _const_spec(shape):
    zeros = (0,) * len(shape)
    return pl.BlockSpec(shape, lambda b, s: zeros, pipeline_mode=pl.Buffered(1))


def _layer(x, mod, g_pre, w_in, conv_w, conv_b, conv_ln_g, conv_ln_b, w_conv_out,
           sgu_ln_g, sgu_ln_b, w_sgu, b_sgu, w_sgu_out, w_o, g_final):
    B, S, D = x.shape
    bf16 = jnp.bfloat16
    shift, scale, gate = (m.reshape(B, 1, D) for m in jnp.split(mod, 3, axis=-1))
    row = lambda v: v.reshape(1, D)
    cw = jnp.pad(conv_w.reshape(CONV_KERNEL, SLABS, LANES).transpose(1, 0, 2),
                 ((0, 0), (0, HALO - CONV_KERNEL), (0, 0)))
    cb = conv_b.reshape(SLABS, 1, LANES)
    bs = jnp.repeat(b_sgu.T, LANES, axis=1)

    per_batch = pl.BlockSpec((None, 1, D), lambda b, s: (b, 0, 0))
    tile = pl.BlockSpec((None, SEQ_TILE, D), lambda b, s: (b, s, 0))
    in_specs = [
        tile, per_batch, per_batch, per_batch,
        _const_spec((1, D)),
        _const_spec(w_in.shape),
        _const_spec(cw.shape), _const_spec(cb.shape),
        _const_spec((1, D)), _const_spec((1, D)),
        _const_spec((D, D)),
        _const_spec((1, D)), _const_spec((1, D)),
        _const_spec(w_sgu.shape), _const_spec(bs.shape),
        _const_spec((D, D)), _const_spec((D, D)),
        _const_spec((1, D)),
    ]
    scratch = [
        pltpu.VMEM((SEQ_TILE, D), bf16),
        pltpu.VMEM((SLABS, A_ROWS, LANES), jnp.float32),
        pltpu.VMEM((SLABS, CONV_ROWS, LANES), jnp.float32),
        pltpu.VMEM((SEQ_TILE, D), jnp.float32),
        pltpu.VMEM((SEQ_TILE, D), jnp.float32),
        pltpu.VMEM((SEQ_TILE, D), jnp.float32),
        pltpu.VMEM((SEQ_TILE, D), bf16),
        pltpu.VMEM((SEQ_TILE, D), bf16),
        pltpu.VMEM((SGU_HEADS, CHUNK, CHUNK), bf16),
    ]
    return pl.pallas_call(
        _layer_kernel,
        out_shape=jax.ShapeDtypeStruct((B, S, D), jnp.float32),
        grid=(B, S // SEQ_TILE),
        in_specs=in_specs,
        out_specs=tile,
        scratch_shapes=scratch,
        compiler_params=pltpu.CompilerParams(
            dimension_semantics=("arbitrary", "arbitrary"),
            vmem_limit_bytes=VMEM_LIMIT_BYTES),
        name="layer",
    )(x, shift, scale, gate, row(g_pre), w_in.astype(bf16), cw, cb,
      row(conv_ln_g), row(conv_ln_b), w_conv_out.astype(bf16),
      row(sgu_ln_g), row(sgu_ln_b), w_sgu, bs,
      w_sgu_out.astype(bf16), w_o.astype(bf16), row(g_final))


def _adaln(c, w, b):
    B, D = c.shape
    N = w.shape[1]
    blk = 512
    return pl.pallas_call(
        _adaln_kernel,
        out_shape=jax.ShapeDtypeStruct((B, N), jnp.float32),
        grid=(N // blk,),
        in_specs=[pl.BlockSpec((B, D), lambda j: (0, 0)),
                  pl.BlockSpec((D, blk), lambda j: (0, j)),
                  pl.BlockSpec((1, blk), lambda j: (0, j))],
        out_specs=pl.BlockSpec((B, blk), lambda j: (0, j)),
        name="adaln",
    )(c, w, b.reshape(1, N))


def kernel(x, c, w_ada, b_ada, g_pre, w_in, conv_w, conv_b, conv_ln_g, conv_ln_b, w_conv_out,
           sgu_ln_g, sgu_ln_b, w_sgu, b_sgu, w_sgu_out, w_o, g_final):
    depth = w_ada.shape[0]
    assert depth == 1, "final norm is fused into the single layer call"
    l = 0
    mod = _adaln(c, w_ada[l], b_ada[l])
    return _layer(x, mod, g_pre[l], w_in[l], conv_w[l], conv_b[l], conv_ln_g[l], conv_ln_b[l],
                  w_conv_out[l], sgu_ln_g[l], sgu_ln_b[l], w_sgu[l], b_sgu[l], w_sgu_out[l],
                  w_o[l], g_final)
```

```python
import functools

import jax
import jax.numpy as jnp
from jax import lax
from jax.experimental import pallas as pl
from jax.experimental.pallas import tpu as pltpu

D_MODEL = 1024
CONV_KERNEL = 31
SGU_HEADS = 8
CHUNK = 128
EPS = 1e-6

LANES = 128
SUBLANES = 8
VMEM_LIMIT_BYTES = 58 * 1024 * 1024

SEQ_TILE = 512
HALO = 32
SLABS = D_MODEL // LANES
SEG = SEQ_TILE // SUBLANES + 1
CONV_ROWS = SEG * SUBLANES
A_ROWS = -(-(CONV_ROWS + HALO) // SUBLANES) * SUBLANES
CONV_JB = 13
assert SEG % CONV_JB == 0 and HALO >= CONV_KERNEL - 1

OFF_A_VAL, OFF_A_GLU, OFF_A_SILU, OFF_B_U, OFF_B_V, OFF_B_SILU, OFF_G_A, OFF_G_B = (
    i * D_MODEL for i in range(8))
COLS = 256
ROWS = 32


def _sigmoid(x):
    return jax.nn.sigmoid(x)


def _silu(x):
    return x * jax.nn.sigmoid(x)


def _gelu(x):
    return 0.5 * x * (1.0 + lax.erf(x * (2.0 ** -0.5)))


def _layernorm(y, g, b):
    mu = jnp.mean(y, axis=-1, keepdims=True)
    yc = y - mu
    var = jnp.mean(yc * yc, axis=-1, keepdims=True)
    return yc * lax.rsqrt(var + EPS) * g + b


def _row_loop(n_rows, block, body):
    for i in range(n_rows // block):
        body(i * block)


def _adaln_kernel(c_ref, w_ref, b_ref, o_ref):
    o_ref[...] = jnp.dot(c_ref[...], w_ref[...], preferred_element_type=jnp.float32,
                         precision=lax.Precision.HIGHEST) + b_ref[...]


def _layer_kernel(x_ref, shift_ref, scale_ref, gate_ref, gpre_ref, win_ref, cw_ref, cb_ref,
                  clg_ref, clb_ref, wco_ref, slg_ref, slb_ref, ws_ref, bs_ref, wso_ref, wo_ref,
                  gfin_ref, o_ref,
                  h_ref, a_ref, yc_ref, f0_ref, f1_ref, f2_ref, g_ref, vn_ref, wt_ref):
    s_idx = pl.program_id(1)
    T = SEQ_TILE
    f32, bf16 = jnp.float32, jnp.bfloat16

    def proj(col, width=COLS):
        return jnp.dot(h_ref[...], win_ref[:, col:col + width], preferred_element_type=f32)

    @pl.when(s_idx == 0)
    def _():
        a_ref[:, 0:HALO, :] = jnp.zeros((SLABS, HALO, LANES), f32)
        a_ref[:, HALO + T:A_ROWS, :] = jnp.zeros((SLABS, A_ROWS - HALO - T, LANES), f32)

    @pl.when(s_idx > 0)
    def _():
        a_ref[:, 0:HALO, :] = a_ref[:, T:T + HALO, :]

    row = lax.broadcasted_iota(jnp.int32, (CHUNK, CHUNK), 0)
    col = lax.broadcasted_iota(jnp.int32, (CHUNK, CHUNK), 1)
    for hd in range(SGU_HEADS):
        wt_ref[hd] = jnp.where(row >= col, ws_ref[hd], 0.0).astype(bf16)

    def prenorm(r):
        x = x_ref[pl.ds(r, ROWS), :]
        y = x * lax.rsqrt(jnp.mean(x * x, axis=-1, keepdims=True) + EPS)
        h = y * gpre_ref[...] * (1.0 + scale_ref[...]) + shift_ref[...]
        h_ref[pl.ds(r, ROWS), :] = h.astype(bf16)
    _row_loop(T, ROWS, prenorm)

    for j in range(D_MODEL // COLS):
        a = proj(OFF_A_VAL + j * COLS) * _sigmoid(proj(OFF_A_GLU + j * COLS))
        for q in range(COLS // LANES):
            a_ref[j * (COLS // LANES) + q, HALO:HALO + T, :] = a[:, q * LANES:(q + 1) * LANES]

    def conv_slab(cb, carry):
        taps = [cw_ref[cb, pl.ds(k, SUBLANES, stride=0), :] for k in range(CONV_KERNEL)]
        bias = cb_ref[cb, pl.ds(0, SUBLANES, stride=0), :]
        for j0 in range(0, SEG, CONV_JB):
            accs = [bias] * CONV_JB
            for k in range(CONV_KERNEL):
                for jj in range(CONV_JB):
                    start = j0 + jj + HALO - (CONV_KERNEL - 1) + k
                    accs[jj] = accs[jj] + taps[k] * a_ref[cb, pl.ds(start, SUBLANES, stride=SEG), :]
            for jj in range(CONV_JB):
                yc_ref[cb, pl.ds(j0 + jj, SUBLANES, stride=SEG), :] = accs[jj]
        return carry
    for cb in range(SLABS):
        conv_slab(cb, None)

    for j in range(D_MODEL // COLS):
        f0_ref[:, j * COLS:(j + 1) * COLS] = proj(OFF_A_SILU + j * COLS)

    def conv_post(r):
        y = jnp.concatenate([yc_ref[cb, pl.ds(r, ROWS), :] for cb in range(SLABS)], axis=-1)
        y = _silu(_layernorm(y, clg_ref[...], clb_ref[...]))
        g_ref[pl.ds(r, ROWS), :] = (y * _silu(f0_ref[pl.ds(r, ROWS), :])).astype(bf16)
    _row_loop(T, ROWS, conv_post)

    for j in range(D_MODEL // COLS):
        f0_ref[:, j * COLS:(j + 1) * COLS] = jnp.dot(
            g_ref[...], wco_ref[:, j * COLS:(j + 1) * COLS], preferred_element_type=f32)

    for j in range(D_MODEL // COLS):
        f1_ref[:, j * COLS:(j + 1) * COLS] = _gelu(proj(OFF_B_U + j * COLS))
        f2_ref[:, j * COLS:(j + 1) * COLS] = _gelu(proj(OFF_B_V + j * COLS))

    def sgu_norm(r):
        v = _layernorm(f2_ref[pl.ds(r, ROWS), :], slg_ref[...], slb_ref[...])
        vn_ref[pl.ds(r, ROWS), :] = v.astype(bf16)
    _row_loop(T, ROWS, sgu_norm)

    for j in range(D_MODEL // COLS):
        f2_ref[:, j * COLS:(j + 1) * COLS] = _silu(proj(OFF_B_SILU + j * COLS))

    def sgu_chunk(r):
        for hd in range(SGU_HEADS):
            cs = slice(hd * LANES, (hd + 1) * LANES)
            mix = jnp.dot(wt_ref[hd], vn_ref[pl.ds(r, CHUNK), cs], preferred_element_type=f32)
            mix = mix + bs_ref[:, cs]
            y = f1_ref[pl.ds(r, CHUNK), cs] * mix * f2_ref[pl.ds(r, CHUNK), cs]
            g_ref[pl.ds(r, CHUNK), cs] = y.astype(bf16)
    _row_loop(T, CHUNK, sgu_chunk)

    for j in range(D_MODEL // COLS):
        f1_ref[:, j * COLS:(j + 1) * COLS] = jnp.dot(
            g_ref[...], wso_ref[:, j * COLS:(j + 1) * COLS], preferred_element_type=f32)

    for j in range(D_MODEL // COLS):
        cs = slice(j * COLS, (j + 1) * COLS)
        merged = (_sigmoid(proj(OFF_G_A + j * COLS)) * f0_ref[:, cs]
                  + _sigmoid(proj(OFF_G_B + j * COLS)) * f1_ref[:, cs])
        g_ref[:, cs] = merged.astype(bf16)

    for j in range(D_MODEL // COLS):
        f2_ref[:, j * COLS:(j + 1) * COLS] = jnp.dot(
            g_ref[...], wo_ref[:, j * COLS:(j + 1) * COLS], preferred_element_type=f32)

    def final(r):
        x = x_ref[pl.ds(r, ROWS), :] + gate_ref[...] * f2_ref[pl.ds(r, ROWS), :]
        y = x * lax.rsqrt(jnp.mean(x * x, axis=-1, keepdims=True) + EPS)
        o_ref[pl.ds(r, ROWS), :] = y * gfin_ref[...]
    _row_loop(T, ROWS, final)


def _const_spec(shape):
    zeros = (0,) * len(shape)
    return pl.BlockSpec(shape, lambda b, s: zeros, pipeline_mode=pl.Buffered(1))


def _layer(x, mod, g_pre, w_in, conv_w, conv_b, conv_ln_g, conv_ln_b, w_conv_out,
           sgu_ln_g, sgu_ln_b, w_sgu, b_sgu, w_sgu_out, w_o, g_final):
    B, S, D = x.shape
    bf16 = jnp.bfloat16
    shift, scale, gate = (m.reshape(B, 1, D) for m in jnp.split(mod, 3, axis=-1))
    row = lambda v: v.reshape(1, D)
    cw = jnp.pad(conv_w.reshape(CONV_KERNEL, SLABS, LANES).transpose(1, 0, 2),
                 ((0, 0), (0, HALO - CONV_KERNEL), (0, 0)))
    cb = conv_b.reshape(SLABS, 1, LANES)
    bs = jnp.repeat(b_sgu.T, LANES, axis=1)

    per_batch = pl.BlockSpec((None, 1, D), lambda b, s: (b, 0, 0))
    tile = pl.BlockSpec((None, SEQ_TILE, D), lambda b, s: (b, s, 0))
    in_specs = [
        tile, per_batch, per_batch, per_batch,
        _const_spec((1, D)),
        _const_spec(w_in.shape),
        _const_spec(cw.shape), _const_spec(cb.shape),
        _const_spec((1, D)), _const_spec((1, D)),
        _const_spec((D, D)),
        _const_spec((1, D)), _const_spec((1, D)),
        _const_spec(w_sgu.shape), _const_spec(bs.shape),
        _const_spec((D, D)), _const_spec((D, D)),
        _const_spec((1, D)),
    ]
    scratch = [
        pltpu.VMEM((SEQ_TILE, D), bf16),
        pltpu.VMEM((SLABS, A_ROWS, LANES), jnp.float32),
        pltpu.VMEM((SLABS, CONV_ROWS, LANES), jnp.float32),
        pltpu.VMEM((SEQ_TILE, D), jnp.float32),
        pltpu.VMEM((SEQ_TILE, D), jnp.float32),
        pltpu.VMEM((SEQ_TILE, D), jnp.float32),
        pltpu.VMEM((SEQ_TILE, D), bf16),
        pltpu.VMEM((SEQ_TILE, D), bf16),
        pltpu.VMEM((SGU_HEADS, CHUNK, CHUNK), bf16),
    ]
    return pl.pallas_call(
        _layer_kernel,
        out_shape=jax.ShapeDtypeStruct((B, S, D), jnp.float32),
        grid=(B, S // SEQ_TILE),
        in_specs=in_specs,
        out_specs=tile,
        scratch_shapes=scratch,
        compiler_params=pltpu.CompilerParams(
            dimension_semantics=("arbitrary", "arbitrary"),
            vmem_limit_bytes=VMEM_LIMIT_BYTES),
        name="layer",
    )(x, shift, scale, gate, row(g_pre), w_in.astype(bf16), cw, cb,
      row(conv_ln_g), row(conv_ln_b), w_conv_out.astype(bf16),
      row(sgu_ln_g), row(sgu_ln_b), w_sgu, bs,
      w_sgu_out.astype(bf16), w_o.astype(bf16), row(g_final))


def _adaln(c, w, b):
    B, D = c.shape
    N = w.shape[1]
    blk = 512
    return pl.pallas_call(
        _adaln_kernel,
        out_shape=jax.ShapeDtypeStruct((B, N), jnp.float32),
        grid=(N // blk,),
        in_specs=[pl.BlockSpec((B, D), lambda j: (0, 0)),
                  pl.BlockSpec((D, blk), lambda j: (0, j)),
                  pl.BlockSpec((1, blk), lambda j: (0, j))],
        out_specs=pl.BlockSpec((B, blk), lambda j: (0, j)),
        name="adaln",
    )(c, w, b.reshape(1, N))


def kernel(x, c, w_ada, b_ada, g_pre, w_in, conv_w, conv_b, conv_ln_g, conv_ln_b, w_conv_out,
           sgu_ln_g, sgu_ln_b, w_sgu, b_sgu, w_sgu_out, w_o, g_final):
    depth = w_ada.shape[0]
    assert depth == 1, "final norm is fused into the single layer call"
    l = 0
    mod = _adaln(c, w_ada[l], b_ada[l])
    return _layer(x, mod, g_pre[l], w_in[l], conv_w[l], conv_b[l], conv_ln_g[l], conv_ln_b[l],
                  w_conv_out[l], sgu_ln_g[l], sgu_ln_b[l], w_sgu[l], b_sgu[l], w_sgu_out[l],
                  w_o[l], g_final)
```

```python
import jax
import jax.numpy as jnp
from jax import lax
from jax.experimental import pallas as pl
from jax.experimental.pallas import tpu as pltpu

D_MODEL = 1024
CONV_KERNEL = 31
SGU_HEADS = 8
CHUNK = 128
EPS = 1e-6

LANES = 128
SUBLANES = 8
VMEM_LIMIT_BYTES = 58 * 1024 * 1024

SEQ_TILE = 256
HALO = 32
SLABS = D_MODEL // LANES
SEG = SEQ_TILE // SUBLANES + 1
CONV_ROWS = SEG * SUBLANES
A_ROWS = -(-(CONV_ROWS + HALO) // SUBLANES) * SUBLANES
CONV_JB = 11
assert SEG % CONV_JB == 0 and HALO >= CONV_KERNEL - 1

OFF_A_VAL, OFF_A_GLU, OFF_A_SILU, OFF_B_U, OFF_B_V, OFF_B_SILU, OFF_G_A, OFF_G_B = (
    i * D_MODEL for i in range(8))
COLS = 256
ROWS = 32


def _sigmoid(x):
    return jax.nn.sigmoid(x)


def _silu(x):
    return x * jax.nn.sigmoid(x)


def _gelu(x):
    return 0.5 * x * (1.0 + lax.erf(x * (2.0 ** -0.5)))


def _layernorm(y, g, b):
    mu = jnp.mean(y, axis=-1, keepdims=True)
    yc = y - mu
    var = jnp.mean(yc * yc, axis=-1, keepdims=True)
    return yc * lax.rsqrt(var + EPS) * g + b


def _adaln_kernel(c_ref, w_ref, b_ref, o_ref):
    o_ref[...] = jnp.dot(c_ref[...], w_ref[...], preferred_element_type=jnp.float32,
                         precision=lax.Precision.HIGHEST) + b_ref[...]


def _zip_tasks(major, minor):
    n, m = len(major), len(minor)
    done = 0
    for i, task in enumerate(major):
        task()
        upto = ((i + 1) * m) // n
        for t in minor[done:upto]:
            t()
        done = upto


def _layer_kernel(x_ref, shift_ref, scale_ref, gate_ref, gpre_ref, win_ref, cw_ref, cb_ref,
                  clg_ref, clb_ref, wco_ref, slg_ref, slb_ref, ws_ref, bs_ref, wso_ref, wo_ref,
                  gfin_ref, o_ref,
                  h_ref, a_ref, yc_ref, za_ref, u_ref, v_ref, zb_ref, sa_ref, sb_ref,
                  ya_ref, yb_ref, out_ref, ga_ref, gb_ref, vn_ref, m_ref, wt_ref):
    s_idx = pl.program_id(1)
    T = SEQ_TILE
    f32, bf16 = jnp.float32, jnp.bfloat16
    NB = D_MODEL // COLS

    def cols(j):
        return slice(j * COLS, (j + 1) * COLS)

    def proj(off, j):
        return jnp.dot(h_ref[...], win_ref[:, off + j * COLS:off + (j + 1) * COLS],
                       preferred_element_type=f32)

    @pl.when(s_idx == 0)
    def _():
        a_ref[:, 0:HALO, :] = jnp.zeros((SLABS, HALO, LANES), f32)
        a_ref[:, HALO + T:A_ROWS, :] = jnp.zeros((SLABS, A_ROWS - HALO - T, LANES), f32)

    @pl.when(s_idx > 0)
    def _():
        a_ref[:, 0:HALO, :] = a_ref[:, T:T + HALO, :]

    row = lax.broadcasted_iota(jnp.int32, (CHUNK, CHUNK), 0)
    col = lax.broadcasted_iota(jnp.int32, (CHUNK, CHUNK), 1)
    for hd in range(SGU_HEADS):
        wt_ref[hd] = jnp.where(row >= col, ws_ref[hd], 0.0).astype(bf16)

    def prenorm(r):
        def task():
            x = x_ref[pl.ds(r, ROWS), :]
            y = x * lax.rsqrt(jnp.mean(x * x, axis=-1, keepdims=True) + EPS)
            h = y * gpre_ref[...] * (1.0 + scale_ref[...]) + shift_ref[...]
            h_ref[pl.ds(r, ROWS), :] = h.astype(bf16)
        return task

    def glu(j):
        def task():
            a = proj(OFF_A_VAL, j) * _sigmoid(proj(OFF_A_GLU, j))
            for q in range(COLS // LANES):
                a_ref[j * (COLS // LANES) + q, HALO:HALO + T, :] = a[:, q * LANES:(q + 1) * LANES]
        return task

    def conv(cb, j0):
        def task():
            taps = [cw_ref[cb, pl.ds(k, SUBLANES, stride=0), :] for k in range(CONV_KERNEL)]
            accs = [cb_ref[cb, pl.ds(0, SUBLANES, stride=0), :]] * CONV_JB
            for k in range(CONV_KERNEL):
                for jj in range(CONV_JB):
                    start = j0 + jj + HALO - (CONV_KERNEL - 1) + k
                    accs[jj] = accs[jj] + taps[k] * a_ref[cb, pl.ds(start, SUBLANES, stride=SEG), :]
            for jj in range(CONV_JB):
                yc_ref[cb, pl.ds(j0 + jj, SUBLANES, stride=SEG), :] = accs[jj]
        return task

    def store_proj(dst_ref, off, j, fn):
        def task():
            dst_ref[:, cols(j)] = fn(proj(off, j))
        return task

    def conv_post(r):
        def task():
            y = jnp.concatenate([yc_ref[cb, pl.ds(r, ROWS), :] for cb in range(SLABS)], axis=-1)
            y = _silu(_layernorm(y, clg_ref[...], clb_ref[...]))
            ga_ref[pl.ds(r, ROWS), :] = (y * _silu(za_ref[pl.ds(r, ROWS), :])).astype(bf16)
        return task

    def sgu_norm(r):
        def task():
            v = _layernorm(v_ref[pl.ds(r, ROWS), :], slg_ref[...], slb_ref[...])
            vn_ref[pl.ds(r, ROWS), :] = v.astype(bf16)
        return task

    def sgu_mix(r, hd):
        def task():
            cs = slice(hd * LANES, (hd + 1) * LANES)
            mix = jnp.dot(wt_ref[hd], vn_ref[pl.ds(r, CHUNK), cs], preferred_element_type=f32)
            mix = mix + bs_ref[:, cs]
            y = u_ref[pl.ds(r, CHUNK), cs] * mix * zb_ref[pl.ds(r, CHUNK), cs]
            gb_ref[pl.ds(r, CHUNK), cs] = y.astype(bf16)
        return task

    def dense(dst_ref, src_ref, w_ref, j):
        def task():
            dst_ref[:, cols(j)] = jnp.dot(src_ref[...], w_ref[:, cols(j)], preferred_element_type=f32)
        return task

    def merge(j):
        def task():
            merged = sa_ref[:, cols(j)] * ya_ref[:, cols(j)] + sb_ref[:, cols(j)] * yb_ref[:, cols(j)]
            m_ref[:, cols(j)] = merged.astype(bf16)
        return task

    def final(r):
        def task():
            x = x_ref[pl.ds(r, ROWS), :] + gate_ref[...] * out_ref[pl.ds(r, ROWS), :]
            y = x * lax.rsqrt(jnp.mean(x * x, axis=-1, keepdims=True) + EPS)
            o_ref[pl.ds(r, ROWS), :] = y * gfin_ref[...]
        return task

    ident = lambda p: p
    row_blocks = range(0, T, ROWS)

    for r in row_blocks:
        prenorm(r)()
    glu(0)()
    mxu = []
    for j in range(1, NB):
        mxu += [glu(j), store_proj(za_ref, OFF_A_SILU, j - 1, ident)]
    mxu += [store_proj(za_ref, OFF_A_SILU, NB - 1, ident)]
    mxu += [store_proj(u_ref, OFF_B_U, j, _gelu) for j in range(NB)]
    mxu += [store_proj(v_ref, OFF_B_V, j, _gelu) for j in range(NB)]
    valu = [conv(cb, j0) for cb in range(SLABS) for j0 in range(0, SEG, CONV_JB)]
    _zip_tasks(valu, mxu)

    mxu = [store_proj(zb_ref, OFF_B_SILU, j, _silu) for j in range(NB)]
    mxu += [store_proj(sa_ref, OFF_G_A, j, _sigmoid) for j in range(NB)]
    mxu += [store_proj(sb_ref, OFF_G_B, j, _sigmoid) for j in range(NB)]
    valu = [conv_post(r) for r in row_blocks] + [sgu_norm(r) for r in row_blocks]
    _zip_tasks(mxu, valu)

    mxu = [dense(ya_ref, ga_ref, wco_ref, j) for j in range(NB)]
    mix = [sgu_mix(r, hd) for r in range(0, T, CHUNK) for hd in range(SGU_HEADS)]
    _zip_tasks(mix, mxu)

    for j in range(NB):
        dense(yb_ref, gb_ref, wso_ref, j)()
        merge(j)()
    for j in range(NB):
        dense(out_ref, m_ref, wo_ref, j)()
    for r in row_blocks:
        final(r)()


def _const_spec(shape):
    zeros = (0,) * len(shape)
    return pl.BlockSpec(shape, lambda b, s: zeros, pipeline_mode=pl.Buffered(1))


def _layer(x, mod, g_pre, w_in, conv_w, conv_b, conv_ln_g, conv_ln_b, w_conv_out,
           sgu_ln_g, sgu_ln_b, w_sgu, b_sgu, w_sgu_out, w_o, g_final):
    B, S, D = x.shape
    bf16 = jnp.bfloat16
    shift, scale, gate = (m.reshape(B, 1, D) for m in jnp.split(mod, 3, axis=-1))
    row = lambda v: v.reshape(1, D)
    cw = jnp.pad(conv_w.reshape(CONV_KERNEL, SLABS, LANES).transpose(1, 0, 2),
                 ((0, 0), (0, HALO - CONV_KERNEL), (0, 0)))
    cb = conv_b.reshape(SLABS, 1, LANES)
    bs = jnp.repeat(b_sgu.T, LANES, axis=1)

    per_batch = pl.BlockSpec((None, 1, D), lambda b, s: (b, 0, 0))
    tile = pl.BlockSpec((None, SEQ_TILE, D), lambda b, s: (b, s, 0))
    in_specs = [
        tile, per_batch, per_batch, per_batch,
        _const_spec((1, D)),
        _const_spec(w_in.shape),
        _const_spec(cw.shape), _const_spec(cb.shape),
        _const_spec((1, D)), _const_spec((1, D)),
        _const_spec((D, D)),
        _const_spec((1, D)), _const_spec((1, D)),
        _const_spec(w_sgu.shape), _const_spec(bs.shape),
        _const_spec((D, D)), _const_spec((D, D)),
        _const_spec((1, D)),
    ]
    scratch = [
        pltpu.VMEM((SEQ_TILE, D), bf16),
        pltpu.VMEM((SLABS, A_ROWS, LANES), jnp.float32),
        pltpu.VMEM((SLABS, CONV_ROWS, LANES), jnp.float32),
    ] + [pltpu.VMEM((SEQ_TILE, D), jnp.float32)] * 9 + [
        pltpu.VMEM((SEQ_TILE, D), bf16),
        pltpu.VMEM((SEQ_TILE, D), bf16),
        pltpu.VMEM((SEQ_TILE, D), bf16),
        pltpu.VMEM((SEQ_TILE, D), bf16),
        pltpu.VMEM((SGU_HEADS, CHUNK, CHUNK), bf16),
    ]
    return pl.pallas_call(
        _layer_kernel,
        out_shape=jax.ShapeDtypeStruct((B, S, D), jnp.float32),
        grid=(B, S // SEQ_TILE),
        in_specs=in_specs,
        out_specs=tile,
        scratch_shapes=scratch,
        compiler_params=pltpu.CompilerParams(
            dimension_semantics=("arbitrary", "arbitrary"),
            vmem_limit_bytes=VMEM_LIMIT_BYTES),
        name="layer",
    )(x, shift, scale, gate, row(g_pre), w_in.astype(bf16), cw, cb,
      row(conv_ln_g), row(conv_ln_b), w_conv_out.astype(bf16),
      row(sgu_ln_g), row(sgu_ln_b), w_sgu, bs,
      w_sgu_out.astype(bf16), w_o.astype(bf16), row(g_final))


def _adaln(c, w, b):
    B, D = c.shape
    N = w.shape[1]
    blk = 512
    return pl.pallas_call(
        _adaln_kernel,
        out_shape=jax.ShapeDtypeStruct((B, N), jnp.float32),
        grid=(N // blk,),
        in_specs=[pl.BlockSpec((B, D), lambda j: (0, 0)),
                  pl.BlockSpec((D, blk), lambda j: (0, j)),
                  pl.BlockSpec((1, blk), lambda j: (0, j))],
        out_specs=pl.BlockSpec((B, blk), lambda j: (0, j)),
        name="adaln",
    )(c, w, b.reshape(1, N))


def kernel(x, c, w_ada, b_ada, g_pre, w_in, conv_w, conv_b, conv_ln_g, conv_ln_b, w_conv_out,
           sgu_ln_g, sgu_ln_b, w_sgu, b_sgu, w_sgu_out, w_o, g_final):
    depth = w_ada.shape[0]
    assert depth == 1, "final norm is fused into the single layer call"
    l = 0
    mod = _adaln(c, w_ada[l], b_ada[l])
    return _layer(x, mod, g_pre[l], w_in[l], conv_w[l], conv_b[l], conv_ln_g[l], conv_ln_b[l],
                  w_conv_out[l], sgu_ln_g[l], sgu_ln_b[l], w_sgu[l], b_sgu[l], w_sgu_out[l],
                  w_o[l], g_final)
```

```python
import jax
import jax.numpy as jnp
from jax import lax
from jax.experimental import pallas as pl
from jax.experimental.pallas import tpu as pltpu

D_MODEL = 1024
CONV_KERNEL = 31
SGU_HEADS = 8
CHUNK = 128
EPS = 1e-6

LANES = 128
SUBLANES = 8
VMEM_LIMIT_BYTES = 58 * 1024 * 1024

SEQ_TILE = 256
HALO = 32
SLABS = D_MODEL // LANES
PACK = 2 * SUBLANES
SEG = SEQ_TILE // PACK + 1
CONV_ROWS = SEG * PACK
TAP0 = HALO - (CONV_KERNEL - 1)
N_TILES = SEG + CONV_KERNEL - 1
A_ROWS = -(-(TAP0 + N_TILES + SEG * (PACK - 1)) // SUBLANES) * SUBLANES
ACCUMULATORS = 4
_N_GROUPS = -(-SEG // ACCUMULATORS)
CONV_GROUPS = [range(g * SEG // _N_GROUPS, (g + 1) * SEG // _N_GROUPS) for g in range(_N_GROUPS)]
assert all(2 <= len(g) <= ACCUMULATORS for g in CONV_GROUPS)
assert SEG % 2 == 1 and HALO >= CONV_KERNEL - 1 and A_ROWS >= HALO + SEQ_TILE

OFF_A_VAL, OFF_A_GLU, OFF_A_SILU, OFF_B_U, OFF_B_V, OFF_B_SILU, OFF_G_A, OFF_G_B = (
    i * D_MODEL for i in range(8))
COLS = 256
ROWS = 32


def _sigmoid(x):
    return jax.nn.sigmoid(x)


def _silu(x):
    return x * jax.nn.sigmoid(x)


def _gelu(x):
    return 0.5 * x * (1.0 + lax.erf(x * (2.0 ** -0.5)))


def _layernorm(y, g, b):
    mu = jnp.mean(y, axis=-1, keepdims=True)
    yc = y - mu
    var = jnp.mean(yc * yc, axis=-1, keepdims=True)
    return yc * lax.rsqrt(var + EPS) * g + b


def _adaln_kernel(c_ref, w_ref, b_ref, o_ref):
    o_ref[...] = jnp.dot(c_ref[...], w_ref[...], preferred_element_type=jnp.float32,
                         precision=lax.Precision.HIGHEST) + b_ref[...]


def _zip_tasks(major, minor):
    n, m = len(major), len(minor)
    done = 0
    for i, task in enumerate(major):
        task()
        upto = ((i + 1) * m) // n
        for t in minor[done:upto]:
            t()
        done = upto


def _layer_kernel(x_ref, shift_ref, scale_ref, gate_ref, gpre_ref, win_ref, cw_ref, cb_ref,
                  clg_ref, clb_ref, wco_ref, slg_ref, slb_ref, ws_ref, bs_ref, wso_ref, wo_ref,
                  gfin_ref, o_ref,
                  h_ref, a_ref, ap_ref, yc_ref, za_ref, u_ref, v_ref, zb_ref, sa_ref, sb_ref,
                  ya_ref, yb_ref, out_ref, ga_ref, gb_ref, vn_ref, m_ref, wt_ref):
    s_idx = pl.program_id(1)
    T = SEQ_TILE
    f32, bf16 = jnp.float32, jnp.bfloat16
    NB = D_MODEL // COLS

    def cols(j):
        return slice(j * COLS, (j + 1) * COLS)

    def proj(off, j):
        return jnp.dot(h_ref[...], win_ref[:, off + j * COLS:off + (j + 1) * COLS],
                       preferred_element_type=f32)

    @pl.when(s_idx == 0)
    def _():
        a_ref[:, 0:HALO, :] = jnp.zeros((SLABS, HALO, LANES), f32)
        a_ref[:, HALO + T:A_ROWS, :] = jnp.zeros((SLABS, A_ROWS - HALO - T, LANES), f32)

    @pl.when(s_idx > 0)
    def _():
        a_ref[:, 0:HALO, :] = a_ref[:, T:T + HALO, :]

    row = lax.broadcasted_iota(jnp.int32, (CHUNK, CHUNK), 0)
    col = lax.broadcasted_iota(jnp.int32, (CHUNK, CHUNK), 1)
    for hd in range(SGU_HEADS):
        wt_ref[hd] = jnp.where(row >= col, ws_ref[hd], 0.0).astype(bf16)

    def prenorm(r):
        def task():
            x = x_ref[pl.ds(r, ROWS), :]
            y = x * lax.rsqrt(jnp.mean(x * x, axis=-1, keepdims=True) + EPS)
            h = y * gpre_ref[...] * (1.0 + scale_ref[...]) + shift_ref[...]
            h_ref[pl.ds(r, ROWS), :] = h.astype(bf16)
        return task

    def glu(j):
        def task():
            a = proj(OFF_A_VAL, j) * _sigmoid(proj(OFF_A_GLU, j))
            for q in range(COLS // LANES):
                a_ref[j * (COLS // LANES) + q, HALO:HALO + T, :] = a[:, q * LANES:(q + 1) * LANES]
        return task

    def pack():
        for cb in range(SLABS):
            for m in range(N_TILES):
                halves = [a_ref[cb, pl.ds(TAP0 + m + half * SUBLANES * SEG, SUBLANES, stride=SEG), :]
                          for half in range(2)]
                ap_ref[cb, m * PACK:(m + 1) * PACK, :] = jnp.concatenate(halves, axis=0).astype(bf16)

    def conv(cb, chains):
        def task():
            accs = {j: jnp.zeros((PACK, LANES), f32) for j in chains}
            for k in range(CONV_KERNEL):
                w = cw_ref[cb, k * PACK:(k + 1) * PACK, :].astype(f32)
                for j in chains:
                    v = ap_ref[cb, (j + k) * PACK:(j + k + 1) * PACK, :]
                    accs[j] = accs[j] + v.astype(f32) * w
            bias = cb_ref[cb, pl.ds(0, SUBLANES, stride=0), :]
            for j in chains:
                for half in range(2):
                    rows = pl.ds(j + half * SUBLANES * SEG, SUBLANES, stride=SEG)
                    yc_ref[cb, rows, :] = accs[j][half * SUBLANES:(half + 1) * SUBLANES, :] + bias
        return task

    def store_proj(dst_ref, off, j, fn):
        def task():
            dst_ref[:, cols(j)] = fn(proj(off, j))
        return task

    def conv_post(r):
        def task():
            y = jnp.concatenate([yc_ref[cb, pl.ds(r, ROWS), :] for cb in range(SLABS)], axis=-1)
            y = _silu(_layernorm(y, clg_ref[...], clb_ref[...]))
            ga_ref[pl.ds(r, ROWS), :] = (y * _silu(za_ref[pl.ds(r, ROWS), :])).astype(bf16)
        return task

    def sgu_norm(r):
        def task():
            v = _layernorm(v_ref[pl.ds(r, ROWS), :], slg_ref[...], slb_ref[...])
            vn_ref[pl.ds(r, ROWS), :] = v.astype(bf16)
        return task

    def sgu_mix(r, hd):
        def task():
            cs = slice(hd * LANES, (hd + 1) * LANES)
            mix = jnp.dot(wt_ref[hd], vn_ref[pl.ds(r, CHUNK), cs], preferred_element_type=f32)
            mix = mix + bs_ref[:, cs]
            y = u_ref[pl.ds(r, CHUNK), cs] * mix * zb_ref[pl.ds(r, CHUNK), cs]
            gb_ref[pl.ds(r, CHUNK), cs] = y.astype(bf16)
        return task

    def dense(dst_ref, src_ref, w_ref, j):
        def task():
            dst_ref[:, cols(j)] = jnp.dot(src_ref[...], w_ref[:, cols(j)], preferred_element_type=f32)
        return task

    def merge(j):
        def task():
            merged = sa_ref[:, cols(j)] * ya_ref[:, cols(j)] + sb_ref[:, cols(j)] * yb_ref[:, cols(j)]
            m_ref[:, cols(j)] = merged.astype(bf16)
        return task

    def final(r):
        def task():
            x = x_ref[pl.ds(r, ROWS), :] + gate_ref[...] * out_ref[pl.ds(r, ROWS), :]
            y = x * lax.rsqrt(jnp.mean(x * x, axis=-1, keepdims=True) + EPS)
            o_ref[pl.ds(r, ROWS), :] = y * gfin_ref[...]
        return task

    ident = lambda p: p
    row_blocks = range(0, T, ROWS)

    always = s_idx >= 0

    for r in row_blocks:
        prenorm(r)()
    for j in range(NB):
        glu(j)()
        store_proj(za_ref, OFF_A_SILU, j, ident)()
    for j in range(NB):
        store_proj(u_ref, OFF_B_U, j, _gelu)()
        store_proj(v_ref, OFF_B_V, j, _gelu)()

    pl.when(always)(pack)

    for cb in range(SLABS):
        for chains in CONV_GROUPS:
            conv(cb, chains)()
    for j in range(NB):
        store_proj(zb_ref, OFF_B_SILU, j, _silu)()

    @pl.when(always)
    def _():
        mxu = [store_proj(sa_ref, OFF_G_A, j, _sigmoid) for j in range(NB)]
        mxu += [store_proj(sb_ref, OFF_G_B, j, _sigmoid) for j in range(NB)]
        valu = [conv_post(r) for r in row_blocks] + [sgu_norm(r) for r in row_blocks]
        _zip_tasks(valu, mxu)

        mxu = [dense(ya_ref, ga_ref, wco_ref, j) for j in range(NB)]
        mix = [sgu_mix(r, hd) for r in range(0, T, CHUNK) for hd in range(SGU_HEADS)]
        _zip_tasks(mix, mxu)

        for j in range(NB):
            dense(yb_ref, gb_ref, wso_ref, j)()
            merge(j)()
        for j in range(NB):
            dense(out_ref, m_ref, wo_ref, j)()
        for r in row_blocks:
            final(r)()


def _const_spec(shape):
    zeros = (0,) * len(shape)
    return pl.BlockSpec(shape, lambda b, s: zeros, pipeline_mode=pl.Buffered(1))


def _layer(x, mod, g_pre, w_in, conv_w, conv_b, conv_ln_g, conv_ln_b, w_conv_out,
           sgu_ln_g, sgu_ln_b, w_sgu, b_sgu, w_sgu_out, w_o, g_final):
    B, S, D = x.shape
    bf16 = jnp.bfloat16
    shift, scale, gate = (m.reshape(B, 1, D) for m in jnp.split(mod, 3, axis=-1))
    row = lambda v: v.reshape(1, D)
    cw = conv_w.astype(bf16).reshape(CONV_KERNEL, SLABS, LANES).transpose(1, 0, 2)
    cw = jnp.repeat(cw, PACK, axis=1)
    cb = conv_b.reshape(SLABS, 1, LANES)
    bs = jnp.repeat(b_sgu.T, LANES, axis=1)

    per_batch = pl.BlockSpec((None, 1, D), lambda b, s: (b, 0, 0))
    tile = pl.BlockSpec((None, SEQ_TILE, D), lambda b, s: (b, s, 0))
    in_specs = [
        tile, per_batch, per_batch, per_batch,
        _const_spec((1, D)),
        _const_spec(w_in.shape),
        _const_spec(cw.shape), _const_spec(cb.shape),
        _const_spec((1, D)), _const_spec((1, D)),
        _const_spec((D, D)),
        _const_spec((1, D)), _const_spec((1, D)),
        _const_spec(w_sgu.shape), _const_spec(bs.shape),
        _const_spec((D, D)), _const_spec((D, D)),
        _const_spec((1, D)),
    ]
    scratch = [
        pltpu.VMEM((SEQ_TILE, D), bf16),
        pltpu.VMEM((SLABS, A_ROWS, LANES), jnp.float32),
        pltpu.VMEM((SLABS, N_TILES * PACK, LANES), bf16),
        pltpu.VMEM((SLABS, CONV_ROWS, LANES), jnp.float32),
    ] + [pltpu.VMEM((SEQ_TILE, D), jnp.float32)] * 9 + [
        pltpu.VMEM((SEQ_TILE, D), bf16),
        pltpu.VMEM((SEQ_TILE, D), bf16),
        pltpu.VMEM((SEQ_TILE, D), bf16),
        pltpu.VMEM((SEQ_TILE, D), bf16),
        pltpu.VMEM((SGU_HEADS, CHUNK, CHUNK), bf16),
    ]
    return pl.pallas_call(
        _layer_kernel,
        out_shape=jax.ShapeDtypeStruct((B, S, D), jnp.float32),
        grid=(B, S // SEQ_TILE),
        in_specs=in_specs,
        out_specs=tile,
        scratch_shapes=scratch,
        compiler_params=pltpu.CompilerParams(
            dimension_semantics=("arbitrary", "arbitrary"),
            vmem_limit_bytes=VMEM_LIMIT_BYTES),
        name="layer",
    )(x, shift, scale, gate, row(g_pre), w_in.astype(bf16), cw, cb,
      row(conv_ln_g), row(conv_ln_b), w_conv_out.astype(bf16),
      row(sgu_ln_g), row(sgu_ln_b), w_sgu, bs,
      w_sgu_out.astype(bf16), w_o.astype(bf16), row(g_final))


def _adaln(c, w, b):
    B, D = c.shape
    N = w.shape[1]
    blk = 512
    return pl.pallas_call(
        _adaln_kernel,
        out_shape=jax.ShapeDtypeStruct((B, N), jnp.float32),
        grid=(N // blk,),
        in_specs=[pl.BlockSpec((B, D), lambda j: (0, 0)),
                  pl.BlockSpec((D, blk), lambda j: (0, j)),
                  pl.BlockSpec((1, blk), lambda j: (0, j))],
        out_specs=pl.BlockSpec((B, blk), lambda j: (0, j)),
        name="adaln",
    )(c, w, b.reshape(1, N))


def kernel(x, c, w_ada, b_ada, g_pre, w_in, conv_w, conv_b, conv_ln_g, conv_ln_b, w_conv_out,
           sgu_ln_g, sgu_ln_b, w_sgu, b_sgu, w_sgu_out, w_o, g_final):
    depth = w_ada.shape[0]
    assert depth == 1, "final norm is fused into the single layer call"
    l = 0
    mod = _adaln(c, w_ada[l], b_ada[l])
    return _layer(x, mod, g_pre[l], w_in[l], conv_w[l], conv_b[l], conv_ln_g[l], conv_ln_b[l],
                  w_conv_out[l], sgu_ln_g[l], sgu_ln_b[l], w_sgu[l], b_sgu[l], w_sgu_out[l],
                  w_o[l], g_final)
```

```python
import functools

import jax
import jax.numpy as jnp
from jax import lax
from jax.experimental import pallas as pl
from jax.experimental.pallas import tpu as pltpu

D_MODEL = 1024
CONV_KERNEL = 31
SGU_HEADS = 8
CHUNK = 128
EPS = 1e-6

LANES = 128
SUBLANES = 8
VMEM_LIMIT_BYTES = 58 * 1024 * 1024

SEQ_TILE = 256
HALO = 32
SLABS = D_MODEL // LANES
PACK = 2 * SUBLANES
SEG = SEQ_TILE // PACK + 1
CONV_ROWS = SEG * PACK
TAP0 = HALO - (CONV_KERNEL - 1)
N_TILES = SEG + CONV_KERNEL - 1
A_ROWS = -(-(TAP0 + N_TILES + SEG * (PACK - 1)) // SUBLANES) * SUBLANES
ACCUMULATORS = 4
CHAINS_IN_FLIGHT = ACCUMULATORS
_N_GROUPS = -(-SEG // CHAINS_IN_FLIGHT)
CONV_GROUPS = [range(g * SEG // _N_GROUPS, (g + 1) * SEG // _N_GROUPS) for g in range(_N_GROUPS)]
assert all(2 <= len(g) <= CHAINS_IN_FLIGHT for g in CONV_GROUPS)
assert SEG % 2 == 1 and HALO >= CONV_KERNEL - 1 and A_ROWS >= HALO + SEQ_TILE

OFF_A_VAL, OFF_A_GLU, OFF_A_SILU, OFF_B_U, OFF_B_V, OFF_B_SILU, OFF_G_A, OFF_G_B = (
    i * D_MODEL for i in range(8))
COLS = 256
ROWS = 32


def _sigmoid(x):
    return jax.nn.sigmoid(x)


def _silu(x):
    return x * jax.nn.sigmoid(x)


def _gelu(x):
    return 0.5 * x * (1.0 + lax.erf(x * (2.0 ** -0.5)))


def _layernorm(y, g, b):
    mu = jnp.mean(y, axis=-1, keepdims=True)
    yc = y - mu
    var = jnp.mean(yc * yc, axis=-1, keepdims=True)
    return yc * lax.rsqrt(var + EPS) * g + b


def _adaln_kernel(c_ref, w_ref, b_ref, o_ref):
    o_ref[...] = jnp.dot(c_ref[...], w_ref[...], preferred_element_type=jnp.float32,
                         precision=lax.Precision.HIGHEST) + b_ref[...]


def _zip_tasks(major, minor):
    n, m = len(major), len(minor)
    done = 0
    for i, task in enumerate(major):
        task()
        upto = ((i + 1) * m) // n
        for t in minor[done:upto]:
            t()
        done = upto


def _layer_kernel(x_ref, xb_ref, shift_ref, scale_ref, gate_ref, gpre_ref, win_ref, cw_ref, cb_ref,
                  clg_ref, clb_ref, wco_ref, slg_ref, slb_ref, ws_ref, bs_ref, wso_ref, wo_ref,
                  gfin_ref, o_ref,
                  h_ref, a_ref, ap_ref, yc_ref, za_ref, u_ref, v_ref, zb_ref, sa_ref, sb_ref,
                  ya_ref, yb_ref, out_ref, ga_ref, gb_ref, vn_ref, m_ref, wt_ref, *, tiles_per_seq):
    i = pl.program_id(0)
    T = SEQ_TILE
    f32, bf16 = jnp.float32, jnp.bfloat16
    NB = D_MODEL // COLS

    def cols(j):
        return slice(j * COLS, (j + 1) * COLS)

    def proj(off, j):
        return jnp.dot(h_ref[...], win_ref[:, off + j * COLS:off + (j + 1) * COLS],
                       preferred_element_type=f32)

    @pl.when(i == 0)
    def _():
        yc_ref[...] = jnp.zeros(yc_ref.shape, f32)
        for ref in (za_ref, u_ref, v_ref, zb_ref, sa_ref, sb_ref):
            ref[...] = jnp.zeros((T, D_MODEL), f32)
        a_ref[:, HALO + T:A_ROWS, :] = jnp.zeros((SLABS, A_ROWS - HALO - T, LANES), f32)
        row = lax.broadcasted_iota(jnp.int32, (CHUNK, CHUNK), 0)
        col = lax.broadcasted_iota(jnp.int32, (CHUNK, CHUNK), 1)
        for hd in range(SGU_HEADS):
            wt_ref[hd] = jnp.where(row >= col, ws_ref[hd], 0.0).astype(bf16)

    seq_start = lax.rem(jnp.minimum(i, pl.num_programs(0) - 2), tiles_per_seq) == 0

    @pl.when(seq_start)
    def _():
        a_ref[:, 0:HALO, :] = jnp.zeros((SLABS, HALO, LANES), f32)

    @pl.when(jnp.logical_not(seq_start))
    def _():
        a_ref[:, 0:HALO, :] = a_ref[:, T:T + HALO, :]

    def prenorm(r):
        def task():
            x = x_ref[pl.ds(r, ROWS), :]
            y = x * lax.rsqrt(jnp.mean(x * x, axis=-1, keepdims=True) + EPS)
            h = y * gpre_ref[...] * (1.0 + scale_ref[...]) + shift_ref[...]
            h_ref[pl.ds(r, ROWS), :] = h.astype(bf16)
        return task

    def glu(j):
        def task():
            a = proj(OFF_A_VAL, j) * _sigmoid(proj(OFF_A_GLU, j))
            for q in range(COLS // LANES):
                a_ref[j * (COLS // LANES) + q, HALO:HALO + T, :] = a[:, q * LANES:(q + 1) * LANES]
        return task

    def pack():
        for cb in range(SLABS):
            for m in range(N_TILES):
                halves = [a_ref[cb, pl.ds(TAP0 + m + half * SUBLANES * SEG, SUBLANES, stride=SEG), :]
                          for half in range(2)]
                ap_ref[cb, m * PACK:(m + 1) * PACK, :] = jnp.concatenate(halves, axis=0).astype(bf16)

    def conv(cb, chains):
        def task():
            accs = {j: jnp.zeros((PACK, LANES), f32) for j in chains}
            for k in range(CONV_KERNEL):
                w = cw_ref[cb, k * PACK:(k + 1) * PACK, :].astype(f32)
                for j in chains:
                    v = ap_ref[cb, (j + k) * PACK:(j + k + 1) * PACK, :]
                    accs[j] = accs[j] + v.astype(f32) * w
            bias = cb_ref[cb, pl.ds(0, SUBLANES, stride=0), :]
            for j in chains:
                for half in range(2):
                    rows = pl.ds(j + half * SUBLANES * SEG, SUBLANES, stride=SEG)
                    yc_ref[cb, rows, :] = accs[j][half * SUBLANES:(half + 1) * SUBLANES, :] + bias
        return task

    def store_proj(dst_ref, off, j, fn):
        def task():
            dst_ref[:, cols(j)] = fn(proj(off, j))
        return task

    def conv_post(r):
        def task():
            y = jnp.concatenate([yc_ref[cb, pl.ds(r, ROWS), :] for cb in range(SLABS)], axis=-1)
            y = _silu(_layernorm(y, clg_ref[...], clb_ref[...]))
            ga_ref[pl.ds(r, ROWS), :] = (y * _silu(za_ref[pl.ds(r, ROWS), :])).astype(bf16)
        return task

    def sgu_norm(r):
        def task():
            v = _layernorm(v_ref[pl.ds(r, ROWS), :], slg_ref[...], slb_ref[...])
            vn_ref[pl.ds(r, ROWS), :] = v.astype(bf16)
        return task

    def sgu_mix(r, hd):
        def task():
            cs = slice(hd * LANES, (hd + 1) * LANES)
            mix = jnp.dot(wt_ref[hd], vn_ref[pl.ds(r, CHUNK), cs], preferred_element_type=f32)
            mix = mix + bs_ref[:, cs]
            y = u_ref[pl.ds(r, CHUNK), cs] * mix * _silu(zb_ref[pl.ds(r, CHUNK), cs])
            gb_ref[pl.ds(r, CHUNK), cs] = y.astype(bf16)
        return task

    def dense(dst_ref, src_ref, w_ref, j):
        def task():
            dst_ref[:, cols(j)] = jnp.dot(src_ref[...], w_ref[:, cols(j)], preferred_element_type=f32)
        return task

    def merge(j):
        def task():
            merged = (_sigmoid(sa_ref[:, cols(j)]) * ya_ref[:, cols(j)]
                      + _sigmoid(sb_ref[:, cols(j)]) * yb_ref[:, cols(j)])
            m_ref[:, cols(j)] = merged.astype(bf16)
        return task

    def final(r):
        def task():
            x = xb_ref[pl.ds(r, ROWS), :] + gate_ref[...] * out_ref[pl.ds(r, ROWS), :]
            y = x * lax.rsqrt(jnp.mean(x * x, axis=-1, keepdims=True) + EPS)
            o_ref[pl.ds(r, ROWS), :] = y * gfin_ref[...]
        return task

    ident = lambda p: p
    row_blocks = range(0, T, ROWS)

    always = i >= 0
    blocks = range(NB)

    for r in row_blocks:
        prenorm(r)()
    _zip_tasks([conv_post(r) for r in row_blocks], [glu(j) for j in blocks])
    _zip_tasks([sgu_norm(r) for r in row_blocks], [dense(ya_ref, ga_ref, wco_ref, j) for j in blocks])
    for r in range(0, T, CHUNK):
        for hd in range(SGU_HEADS):
            sgu_mix(r, hd)()
    for j in blocks:
        store_proj(v_ref, OFF_B_V, j, _gelu)()
        dense(yb_ref, gb_ref, wso_ref, j)()
        merge(j)()
    for j in blocks:
        store_proj(u_ref, OFF_B_U, j, _gelu)()
        dense(out_ref, m_ref, wo_ref, j)()
    for r in row_blocks:
        final(r)()

    pl.when(always)(pack)

    @pl.when(always)
    def _():
        mxu = [store_proj(za_ref, OFF_A_SILU, j, ident) for j in blocks]
        mxu += [store_proj(zb_ref, OFF_B_SILU, j, ident) for j in blocks]
        mxu += [store_proj(sa_ref, OFF_G_A, j, ident) for j in blocks]
        mxu += [store_proj(sb_ref, OFF_G_B, j, ident) for j in blocks]
        _zip_tasks([conv(cb, chains) for cb in range(SLABS) for chains in CONV_GROUPS], mxu)


def _const_spec(shape):
    zeros = (0,) * len(shape)
    return pl.BlockSpec(shape, lambda i: zeros, pipeline_mode=pl.Buffered(1))


def _layer(x, mod, g_pre, w_in, conv_w, conv_b, conv_ln_g, conv_ln_b, w_conv_out,
           sgu_ln_g, sgu_ln_b, w_sgu, b_sgu, w_sgu_out, w_o, g_final):
    B, S, D = x.shape
    bf16 = jnp.bfloat16
    tiles_per_seq = S // SEQ_TILE
    n_tiles = B * tiles_per_seq
    shift, scale, gate = (m.reshape(B, 1, D) for m in jnp.split(mod, 3, axis=-1))
    row = lambda v: v.reshape(1, D)
    cw = conv_w.astype(bf16).reshape(CONV_KERNEL, SLABS, LANES).transpose(1, 0, 2)
    cw = jnp.repeat(cw, PACK, axis=1)
    cb = conv_b.reshape(SLABS, 1, LANES)
    bs = jnp.repeat(b_sgu.T, LANES, axis=1)

    def head(i):
        return jnp.minimum(i, n_tiles - 1)

    def tail(i):
        return jnp.maximum(i - 1, 0)

    def tile_spec(which):
        return pl.BlockSpec((None, SEQ_TILE, D),
                            lambda i: (which(i) // tiles_per_seq, which(i) % tiles_per_seq, 0))

    def batch_spec(which):
        return pl.BlockSpec((None, 1, D), lambda i: (which(i) // tiles_per_seq, 0, 0))

    in_specs = [
        tile_spec(head), tile_spec(tail), batch_spec(head), batch_spec(head), batch_spec(tail),
        _const_spec((1, D)),
        _const_spec(w_in.shape),
        _const_spec(cw.shape), _const_spec(cb.shape),
        _const_spec((1, D)), _const_spec((1, D)),
        _const_spec((D, D)),
        _const_spec((1, D)), _const_spec((1, D)),
        _const_spec(w_sgu.shape), _const_spec(bs.shape),
        _const_spec((D, D)), _const_spec((D, D)),
        _const_spec((1, D)),
    ]
    scratch = [
        pltpu.VMEM((SEQ_TILE, D), bf16),
        pltpu.VMEM((SLABS, A_ROWS, LANES), jnp.float32),
        pltpu.VMEM((SLABS, N_TILES * PACK, LANES), bf16),
        pltpu.VMEM((SLABS, CONV_ROWS, LANES), jnp.float32),
    ] + [pltpu.VMEM((SEQ_TILE, D), jnp.float32)] * 9 + [
        pltpu.VMEM((SEQ_TILE, D), bf16),
        pltpu.VMEM((SEQ_TILE, D), bf16),
        pltpu.VMEM((SEQ_TILE, D), bf16),
        pltpu.VMEM((SEQ_TILE, D), bf16),
        pltpu.VMEM((SGU_HEADS, CHUNK, CHUNK), bf16),
    ]
    return pl.pallas_call(
        functools.partial(_layer_kernel, tiles_per_seq=tiles_per_seq),
        out_shape=jax.ShapeDtypeStruct((B, S, D), jnp.float32),
        grid=(n_tiles + 1,),
        in_specs=in_specs,
        out_specs=tile_spec(tail),
        scratch_shapes=scratch,
        compiler_params=pltpu.CompilerParams(
            dimension_semantics=("arbitrary",),
            vmem_limit_bytes=VMEM_LIMIT_BYTES),
        name="layer",
    )(x, x, shift, scale, gate, row(g_pre), w_in.astype(bf16), cw, cb,
      row(conv_ln_g), row(conv_ln_b), w_conv_out.astype(bf16),
      row(sgu_ln_g), row(sgu_ln_b), w_sgu, bs,
      w_sgu_out.astype(bf16), w_o.astype(bf16), row(g_final))


def _adaln(c, w, b):
    B, D = c.shape
    N = w.shape[1]
    blk = 512
    return pl.pallas_call(
        _adaln_kernel,
        out_shape=jax.ShapeDtypeStruct((B, N), jnp.float32),
        grid=(N // blk,),
        in_specs=[pl.BlockSpec((B, D), lambda j: (0, 0)),
                  pl.BlockSpec((D, blk), lambda j: (0, j)),
                  pl.BlockSpec((1, blk), lambda j: (0, j))],
        out_specs=pl.BlockSpec((B, blk), lambda j: (0, j)),
        name="adaln",
    )(c, w, b.reshape(1, N))


def kernel(x, c, w_ada, b_ada, g_pre, w_in, conv_w, conv_b, conv_ln_g, conv_ln_b, w_conv_out,
           sgu_ln_g, sgu_ln_b, w_sgu, b_sgu, w_sgu_out, w_o, g_final):
    depth = w_ada.shape[0]
    assert depth == 1, "final norm is fused into the single layer call"
    l = 0
    mod = _adaln(c, w_ada[l], b_ada[l])
    return _layer(x, mod, g_pre[l], w_in[l], conv_w[l], conv_b[l], conv_ln_g[l], conv_ln_b[l],
                  w_conv_out[l], sgu_ln_g[l], sgu_ln_b[l], w_sgu[l], b_sgu[l], w_sgu_out[l],
                  w_o[l], g_final)
```

```python
import functools

import jax
import jax.numpy as jnp
from jax import lax
from jax.experimental import pallas as pl
from jax.experimental.pallas import tpu as pltpu

D_MODEL = 1024
CONV_KERNEL = 31
SGU_HEADS = 8
CHUNK = 128
EPS = 1e-6

LANES = 128
SUBLANES = 8
VMEM_LIMIT_BYTES = 58 * 1024 * 1024

SEQ_TILE = 256
HALO = 32
SLABS = D_MODEL // LANES
PACK = 2 * SUBLANES
SEG = SEQ_TILE // PACK + 1
CONV_ROWS = SEG * PACK
TAP0 = HALO - (CONV_KERNEL - 1)
N_TILES = SEG + CONV_KERNEL - 1
A_ROWS = -(-(TAP0 + N_TILES + SEG * (PACK - 1)) // SUBLANES) * SUBLANES
ACCUMULATORS = 4
_N_GROUPS = -(-SEG // ACCUMULATORS)
CONV_GROUPS = [range(g * SEG // _N_GROUPS, (g + 1) * SEG // _N_GROUPS) for g in range(_N_GROUPS)]
assert all(2 <= len(g) <= ACCUMULATORS for g in CONV_GROUPS)
assert SEG % 2 == 1 and HALO >= CONV_KERNEL - 1 and A_ROWS >= HALO + SEQ_TILE

OFF_A_VAL, OFF_A_GLU, OFF_A_SILU, OFF_B_U, OFF_B_V, OFF_B_SILU, OFF_G_A, OFF_G_B = (
    i * D_MODEL for i in range(8))
COLS = 256
ROWS = 32


def _sigmoid(x):
    return jax.nn.sigmoid(x)


def _silu(x):
    return x * jax.nn.sigmoid(x)


def _gelu(x):
    return 0.5 * x * (1.0 + lax.erf(x * (2.0 ** -0.5)))


def _layernorm(y, g, b):
    mu = jnp.mean(y, axis=-1, keepdims=True)
    yc = y - mu
    var = jnp.mean(yc * yc, axis=-1, keepdims=True)
    return yc * lax.rsqrt(var + EPS) * g + b


def _adaln_kernel(c_ref, w_ref, b_ref, o_ref):
    o_ref[...] = jnp.dot(c_ref[...], w_ref[...], preferred_element_type=jnp.float32,
                         precision=lax.Precision.HIGHEST) + b_ref[...]


def _zip_tasks(major, minor):
    n, m = len(major), len(minor)
    done = 0
    for i, task in enumerate(major):
        task()
        upto = ((i + 1) * m) // n
        for t in minor[done:upto]:
            t()
        done = upto


def _layer_kernel(x_ref, shift_ref, scale_ref, gate_ref, gpre_ref, win_ref, cw_ref, cb_ref,
                  clg_ref, clb_ref, wco_ref, slg_ref, slb_ref, ws_ref, bs_ref, wso_ref, wo_ref,
                  gfin_ref, o_ref,
                  h_ref, a_ref, ap_ref, yc_ref, za_ref, u_ref, v_ref, zb_ref, sa_ref, sb_ref,
                  ya_ref, yb_ref, out_ref, ga_ref, gb_ref, vn_ref, m_ref, wt_ref, xp_ref, *, tiles_per_seq):
    i = pl.program_id(0)
    T = SEQ_TILE
    f32, bf16 = jnp.float32, jnp.bfloat16
    NB = D_MODEL // COLS

    def cols(j):
        return slice(j * COLS, (j + 1) * COLS)

    def proj(off, j):
        return jnp.dot(h_ref[...], win_ref[:, off + j * COLS:off + (j + 1) * COLS],
                       preferred_element_type=f32)

    @pl.when(i == 0)
    def _():
        yc_ref[...] = jnp.zeros(yc_ref.shape, f32)
        for ref in (za_ref, u_ref, v_ref, zb_ref, sa_ref, sb_ref, xp_ref):
            ref[...] = jnp.zeros((T, D_MODEL), f32)
        a_ref[:, HALO + T:A_ROWS, :] = jnp.zeros((SLABS, A_ROWS - HALO - T, LANES), f32)
        row = lax.broadcasted_iota(jnp.int32, (CHUNK, CHUNK), 0)
        col = lax.broadcasted_iota(jnp.int32, (CHUNK, CHUNK), 1)
        for hd in range(SGU_HEADS):
            wt_ref[hd] = jnp.where(row >= col, ws_ref[hd], 0.0).astype(bf16)

    seq_start = lax.rem(jnp.minimum(i, pl.num_programs(0) - 2), tiles_per_seq) == 0

    @pl.when(seq_start)
    def _():
        a_ref[:, 0:HALO, :] = jnp.zeros((SLABS, HALO, LANES), f32)

    @pl.when(jnp.logical_not(seq_start))
    def _():
        a_ref[:, 0:HALO, :] = a_ref[:, T:T + HALO, :]

    def prenorm(r):
        def task():
            x = x_ref[pl.ds(r, ROWS), :]
            y = x * lax.rsqrt(jnp.mean(x * x, axis=-1, keepdims=True) + EPS)
            h = y * gpre_ref[...] * (1.0 + scale_ref[...]) + shift_ref[...]
            h_ref[pl.ds(r, ROWS), :] = h.astype(bf16)
        return task

    def glu(j):
        def task():
            a = proj(OFF_A_VAL, j) * _sigmoid(proj(OFF_A_GLU, j))
            for q in range(COLS // LANES):
                a_ref[j * (COLS // LANES) + q, HALO:HALO + T, :] = a[:, q * LANES:(q + 1) * LANES]
        return task

    def pack():
        for cb in range(SLABS):
            for m in range(N_TILES):
                halves = [a_ref[cb, pl.ds(TAP0 + m + half * SUBLANES * SEG, SUBLANES, stride=SEG), :]
                          for half in range(2)]
                ap_ref[cb, m * PACK:(m + 1) * PACK, :] = jnp.concatenate(halves, axis=0).astype(bf16)

    def conv(cb, chains):
        def task():
            accs = {j: jnp.zeros((PACK, LANES), f32) for j in chains}
            for k in range(CONV_KERNEL):
                w = cw_ref[cb, k * PACK:(k + 1) * PACK, :].astype(f32)
                for j in chains:
                    v = ap_ref[cb, (j + k) * PACK:(j + k + 1) * PACK, :]
                    accs[j] = accs[j] + v.astype(f32) * w
            bias = cb_ref[cb, pl.ds(0, SUBLANES, stride=0), :]
            for j in chains:
                for half in range(2):
                    rows = pl.ds(j + half * SUBLANES * SEG, SUBLANES, stride=SEG)
                    yc_ref[cb, rows, :] = accs[j][half * SUBLANES:(half + 1) * SUBLANES, :] + bias
        return task

    def store_proj(dst_ref, off, j, fn):
        def task():
            dst_ref[:, cols(j)] = fn(proj(off, j))
        return task

    def conv_post(r):
        def task():
            y = jnp.concatenate([yc_ref[cb, pl.ds(r, ROWS), :] for cb in range(SLABS)], axis=-1)
            y = _silu(_layernorm(y, clg_ref[...], clb_ref[...]))
            ga_ref[pl.ds(r, ROWS), :] = (y * _silu(za_ref[pl.ds(r, ROWS), :])).astype(bf16)
        return task

    def sgu_norm(r):
        def task():
            v = _layernorm(v_ref[pl.ds(r, ROWS), :], slg_ref[...], slb_ref[...])
            vn_ref[pl.ds(r, ROWS), :] = v.astype(bf16)
        return task

    def sgu_mix(r, hd):
        def task():
            cs = slice(hd * LANES, (hd + 1) * LANES)
            mix = jnp.dot(wt_ref[hd], vn_ref[pl.ds(r, CHUNK), cs], preferred_element_type=f32)
            mix = mix + bs_ref[:, cs]
            y = u_ref[pl.ds(r, CHUNK), cs] * mix * _silu(zb_ref[pl.ds(r, CHUNK), cs])
            gb_ref[pl.ds(r, CHUNK), cs] = y.astype(bf16)
        return task

    def dense(dst_ref, src_ref, w_ref, j):
        def task():
            dst_ref[:, cols(j)] = jnp.dot(src_ref[...], w_ref[:, cols(j)], preferred_element_type=f32)
        return task

    def merge(j):
        def task():
            merged = (_sigmoid(sa_ref[:, cols(j)]) * ya_ref[:, cols(j)]
                      + _sigmoid(sb_ref[:, cols(j)]) * yb_ref[:, cols(j)])
            m_ref[:, cols(j)] = merged.astype(bf16)
        return task

    def final(r):
        def task():
            x = xp_ref[pl.ds(r, ROWS), :] + gate_ref[...] * out_ref[pl.ds(r, ROWS), :]
            y = x * lax.rsqrt(jnp.mean(x * x, axis=-1, keepdims=True) + EPS)
            o_ref[pl.ds(r, ROWS), :] = y * gfin_ref[...]
        return task

    ident = lambda p: p
    row_blocks = range(0, T, ROWS)

    always = i >= 0
    blocks = range(NB)

    for r in row_blocks:
        prenorm(r)()
    _zip_tasks([conv_post(r) for r in row_blocks], [glu(j) for j in blocks])
    _zip_tasks([sgu_norm(r) for r in row_blocks], [dense(ya_ref, ga_ref, wco_ref, j) for j in blocks])
    for r in range(0, T, CHUNK):
        for hd in range(SGU_HEADS):
            sgu_mix(r, hd)()
    for j in blocks:
        store_proj(v_ref, OFF_B_V, j, _gelu)()
        dense(yb_ref, gb_ref, wso_ref, j)()
        merge(j)()
    for j in blocks:
        store_proj(u_ref, OFF_B_U, j, _gelu)()
        dense(out_ref, m_ref, wo_ref, j)()
    for r in row_blocks:
        final(r)()
    xp_ref[...] = x_ref[...]

    pl.when(always)(pack)

    @pl.when(always)
    def _():
        mxu = [store_proj(za_ref, OFF_A_SILU, j, ident) for j in blocks]
        mxu += [store_proj(zb_ref, OFF_B_SILU, j, ident) for j in blocks]
        mxu += [store_proj(sa_ref, OFF_G_A, j, ident) for j in blocks]
        mxu += [store_proj(sb_ref, OFF_G_B, j, ident) for j in blocks]
        _zip_tasks([conv(cb, chains) for cb in range(SLABS) for chains in CONV_GROUPS], mxu)


def _const_spec(shape):
    zeros = (0,) * len(shape)
    return pl.BlockSpec(shape, lambda i: zeros, pipeline_mode=pl.Buffered(1))


def _layer(x, mod, g_pre, w_in, conv_w, conv_b, conv_ln_g, conv_ln_b, w_conv_out,
           sgu_ln_g, sgu_ln_b, w_sgu, b_sgu, w_sgu_out, w_o, g_final):
    B, S, D = x.shape
    bf16 = jnp.bfloat16
    tiles_per_seq = S // SEQ_TILE
    n_tiles = B * tiles_per_seq
    shift, scale, gate = (m.reshape(B, 1, D) for m in jnp.split(mod, 3, axis=-1))
    row = lambda v: v.reshape(1, D)
    cw = conv_w.astype(bf16).reshape(CONV_KERNEL, SLABS, LANES).transpose(1, 0, 2)
    cw = jnp.repeat(cw, PACK, axis=1)
    cb = conv_b.reshape(SLABS, 1, LANES)
    bs = jnp.repeat(b_sgu.T, LANES, axis=1)

    def head(i):
        return jnp.minimum(i, n_tiles - 1)

    def tail(i):
        return jnp.maximum(i - 1, 0)

    def tile_spec(which):
        return pl.BlockSpec((None, SEQ_TILE, D),
                            lambda i: (which(i) // tiles_per_seq, which(i) % tiles_per_seq, 0))

    def batch_spec(which):
        return pl.BlockSpec((None, 1, D), lambda i: (which(i) // tiles_per_seq, 0, 0))

    in_specs = [
        tile_spec(head), batch_spec(head), batch_spec(head), batch_spec(tail),
        _const_spec((1, D)),
        _const_spec(w_in.shape),
        _const_spec(cw.shape), _const_spec(cb.shape),
        _const_spec((1, D)), _const_spec((1, D)),
        _const_spec((D, D)),
        _const_spec((1, D)), _const_spec((1, D)),
        _const_spec(w_sgu.shape), _const_spec(bs.shape),
        _const_spec((D, D)), _const_spec((D, D)),
        _const_spec((1, D)),
    ]
    scratch = [
        pltpu.VMEM((SEQ_TILE, D), bf16),
        pltpu.VMEM((SLABS, A_ROWS, LANES), jnp.float32),
        pltpu.VMEM((SLABS, N_TILES * PACK, LANES), bf16),
        pltpu.VMEM((SLABS, CONV_ROWS, LANES), jnp.float32),
    ] + [pltpu.VMEM((SEQ_TILE, D), jnp.float32)] * 9 + [
        pltpu.VMEM((SEQ_TILE, D), bf16),
        pltpu.VMEM((SEQ_TILE, D), bf16),
        pltpu.VMEM((SEQ_TILE, D), bf16),
        pltpu.VMEM((SEQ_TILE, D), bf16),
        pltpu.VMEM((SGU_HEADS, CHUNK, CHUNK), bf16),
        pltpu.VMEM((SEQ_TILE, D), jnp.float32),
    ]
    return pl.pallas_call(
        functools.partial(_layer_kernel, tiles_per_seq=tiles_per_seq),
        out_shape=jax.ShapeDtypeStruct((B, S, D), jnp.float32),
        grid=(n_tiles + 1,),
        in_specs=in_specs,
        out_specs=tile_spec(tail),
        scratch_shapes=scratch,
        compiler_params=pltpu.CompilerParams(
            dimension_semantics=("arbitrary",),
            vmem_limit_bytes=VMEM_LIMIT_BYTES),
        name="layer",
    )(x, shift, scale, gate, row(g_pre), w_in.astype(bf16), cw, cb,
      row(conv_ln_g), row(conv_ln_b), w_conv_out.astype(bf16),
      row(sgu_ln_g), row(sgu_ln_b), w_sgu, bs,
      w_sgu_out.astype(bf16), w_o.astype(bf16), row(g_final))


def _adaln(c, w, b):
    B, D = c.shape
    N = w.shape[1]
    blk = 512
    return pl.pallas_call(
        _adaln_kernel,
        out_shape=jax.ShapeDtypeStruct((B, N), jnp.float32),
        grid=(N // blk,),
        in_specs=[pl.BlockSpec((B, D), lambda j: (0, 0)),
                  pl.BlockSpec((D, blk), lambda j: (0, j)),
                  pl.BlockSpec((1, blk), lambda j: (0, j))],
        out_specs=pl.BlockSpec((B, blk), lambda j: (0, j)),
        name="adaln",
    )(c, w, b.reshape(1, N))


def kernel(x, c, w_ada, b_ada, g_pre, w_in, conv_w, conv_b, conv_ln_g, conv_ln_b, w_conv_out,
           sgu_ln_g, sgu_ln_b, w_sgu, b_sgu, w_sgu_out, w_o, g_final):
    depth = w_ada.shape[0]
    assert depth == 1, "final norm is fused into the single layer call"
    l = 0
    mod = _adaln(c, w_ada[l], b_ada[l])
    return _layer(x, mod, g_pre[l], w_in[l], conv_w[l], conv_b[l], conv_ln_g[l], conv_ln_b[l],
                  w_conv_out[l], sgu_ln_g[l], sgu_ln_b[l], w_sgu[l], b_sgu[l], w_sgu_out[l],
                  w_o[l], g_final)
```
